```python
import math
import jax, jax.numpy as jnp
from jax import lax
import numpy as np

D_MODEL = 1024
BATCH = 8
SEQ = 2048
DEPTH = 1

ATT_HEADS = 12
ATT_HEAD_DIM = 64
ATT_WIDTH = ATT_HEADS * ATT_HEAD_DIM
DILATED_PATTERNS = ((128, 1), (512, 4), (2048, 16))
ATT_BLOCK = 128
ROPE_THETA = 10000.0

DN_HEADS = 6
DN_HEAD_DIM = 128
DN_WIDTH = DN_HEADS * DN_HEAD_DIM
CONV_WIDTH = 4
DN_CHUNK = 64

N_BRANCHES = 2

SPLIT_SIZES = (ATT_WIDTH, ATT_WIDTH, ATT_WIDTH,
               3 * DN_WIDTH,
               DN_HEADS,
               DN_HEADS,
               DN_WIDTH,
               N_BRANCHES * D_MODEL)
IN_COLS = sum(SPLIT_SIZES)

PEER_HEADS = 8
N_KEYS = 128
N_EXPERTS = N_KEYS * N_KEYS
PEER_QUERY_DIM = 256
PEER_HALF = PEER_QUERY_DIM // 2
PEER_TOPK = 16
PEER_TOKEN_BLOCK = 128

ALPHA = (2.0 * DEPTH) ** 0.25
BETA = (8.0 * DEPTH) ** -0.25
LN_EPS = 1e-5
RMS_EPS = 1e-6
NEG = -1e30

kernel_name = "hybrid_dilated_attn_gated_deltanet_peer_deepnorm"

f32 = jnp.float32


def layer_norm(x, gain, bias):
    xf = x.astype(f32)
    mu = jnp.mean(xf, axis=-1, keepdims=True)
    var = jnp.mean(jnp.square(xf - mu), axis=-1, keepdims=True)
    return ((xf - mu) * lax.rsqrt(var + LN_EPS) * gain.astype(f32) + bias.astype(f32)).astype(x.dtype)


def rope(x):
    S, d = x.shape[1], x.shape[-1]
    inv_freq = ROPE_THETA ** (-jnp.arange(0, d, 2, dtype=f32) / d)
    ang = jnp.arange(S, dtype=f32)[:, None] * inv_freq[None, :]
    cos = jnp.cos(ang)[None, :, None, :]
    sin = jnp.sin(ang)[None, :, None, :]
    x1, x2 = x[..., : d // 2], x[..., d // 2:]
    return jnp.concatenate([x1 * cos - x2 * sin, x1 * sin + x2 * cos], axis=-1)


def dilated_band_attention(q, k, v, dilation, band):
    B, S, H, d = q.shape
    L = S // dilation
    nb = -(-L // ATT_BLOCK)
    Lp = nb * ATT_BLOCK

    def to_sub(t, front):
        t = t.reshape(B, L, dilation, H, d).transpose(0, 2, 1, 3, 4)
        return jnp.pad(t, ((0, 0), (0, 0), (front, Lp - L), (0, 0), (0, 0)))

    qs = to_sub(q, 0).reshape(B, dilation, nb, ATT_BLOCK, H, d)

    def band_keys(t):
        tp = to_sub(t, ATT_BLOCK)
        prev = tp[:, :, :Lp].reshape(B, dilation, nb, ATT_BLOCK, H, d)
        cur = tp[:, :, ATT_BLOCK:].reshape(B, dilation, nb, ATT_BLOCK, H, d)
        return jnp.concatenate([prev, cur], axis=3)

    kb, vb = band_keys(k), band_keys(v)
    s = jnp.einsum('brnqhd,brnkhd->brnhqk', qs, kb) * (d ** -0.5)
    qi = jnp.arange(ATT_BLOCK)[:, None]
    kj = jnp.arange(2 * ATT_BLOCK)[None, :]
    dist = ATT_BLOCK + qi - kj
    key_pos = jnp.arange(nb)[:, None, None] * ATT_BLOCK + kj[None] - ATT_BLOCK
    mask = (dist >= 0) & (dist <= band) & (key_pos >= 0)
    s = jnp.where(mask[:, None], s, NEG)
    m = jnp.max(s, axis=-1, keepdims=True)
    lse = m[..., 0] + jnp.log(jnp.sum(jnp.exp(s - m), axis=-1))
    p = jnp.exp(s - lse[..., None])
    o = jnp.einsum('brnhqk,brnkhd->brnqhd', p, vb)
    o = o.reshape(B, dilation, Lp, H, d)[:, :, :L].transpose(0, 2, 1, 3, 4).reshape(B, S, H, d)
    lse = lse.transpose(0, 1, 2, 4, 3).reshape(B, dilation, Lp, H)[:, :, :L]
    lse = lse.transpose(0, 2, 1, 3).reshape(B, S, H)
    return o, lse


def dilated_attention(q, k, v):
    outs, lses = [], []
    for window, dilation in DILATED_PATTERNS:
        o, l = dilated_band_attention(q, k, v, dilation, window // dilation)
        outs.append(o)
        lses.append(l)
    outs = jnp.stack(outs, axis=3)
    wts = jax.nn.softmax(jnp.stack(lses, axis=-1), axis=-1)
    return jnp.einsum('bshg,bshgd->bshd', wts, outs)


def causal_depthwise_conv(x, w):
    return lax.conv_general_dilated(x, w[:, None, :], window_strides=(1,),
                                    padding=[(CONV_WIDTH - 1, 0)],
                                    dimension_numbers=('NWC', 'WIO', 'NWC'),
                                    feature_group_count=x.shape[-1])


def l2norm(x):
    return x * lax.rsqrt(jnp.sum(jnp.square(x), axis=-1, keepdims=True) + RMS_EPS)


def gated_delta_rule(q, k, v, g, beta):
    B, S, H, dk = q.shape
    dv = v.shape[-1]
    C = DN_CHUNK
    n = S // C

    def chunks(t):
        return jnp.swapaxes(t.reshape((B, n, C, H) + t.shape[3:]), 2, 3)

    qc, kc, vc, gc, bc = chunks(q), chunks(k), chunks(v), chunks(g), chunks(beta)
    G = jnp.cumsum(gc, axis=-1)
    idx = jnp.arange(C)
    lower = idx[:, None] >= idx[None, :]
    strict = idx[:, None] > idx[None, :]
    decay_mat = jnp.exp(jnp.where(lower, G[..., :, None] - G[..., None, :], NEG))
    kk = jnp.einsum('bnhid,bnhjd->bnhij', kc, kc)
    M = jnp.where(strict, bc[..., None] * kk * decay_mat, 0.0)
    eye = jnp.eye(C, dtype=f32)
    T = lax.linalg.triangular_solve(eye + M, jnp.broadcast_to(eye, M.shape), left_side=True,
                                    lower=True, unit_diagonal=True)
    u = T @ (bc[..., None] * vc)
    w = T @ (bc[..., None] * kc * jnp.exp(G)[..., None])
    qk = jnp.einsum('bnhid,bnhjd->bnhij', qc, kc) * decay_mat
    q_dec = qc * jnp.exp(G)[..., None]
    k_dec = kc * jnp.exp(G[..., -1:] - G)[..., None]
    g_last = jnp.exp(G[..., -1])

    def step(state, inp):
        u_i, w_i, qk_i, qd_i, kd_i, gl_i = inp
        v_new = u_i - jnp.einsum('bhcd,bhde->bhce', w_i, state)
        o_i = jnp.einsum('bhcd,bhde->bhce', qd_i, state) + jnp.einsum('bhij,bhje->bhie', qk_i, v_new)
        state = gl_i[..., None, None] * state + jnp.einsum('bhcd,bhce->bhde', kd_i, v_new)
        return state, o_i

    xs = (u, w, qk, q_dec, k_dec, g_last)
    xs = tuple(jnp.moveaxis(t, 1, 0) for t in xs)
    _, o = lax.scan(step, jnp.zeros((B, H, dk, dv), f32), xs)
    o = jnp.moveaxis(o, 0, 1)
    return jnp.swapaxes(o, 2, 3).reshape(B, S, H, dv)


def mixing_sublayer(x, w_in, dn_conv, dn_A_log, dn_dt_bias, dn_norm_gain,
                    w_branch_attn, w_branch_delta, w_out):
    B, S, _ = x.shape
    h = x @ w_in
    points, acc = [], 0
    for sz in SPLIT_SIZES[:-1]:
        acc += sz
        points.append(acc)
    q_a, k_a, v_a, qkv_b, beta_in, a_in, gate_b, br_gate = jnp.split(h, points, axis=-1)

    qa = rope(q_a.astype(f32).reshape(B, S, ATT_HEADS, ATT_HEAD_DIM))
    ka = rope(k_a.astype(f32).reshape(B, S, ATT_HEADS, ATT_HEAD_DIM))
    va = v_a.astype(f32).reshape(B, S, ATT_HEADS, ATT_HEAD_DIM)
    y_a = dilated_attention(qa, ka, va).reshape(B, S, ATT_WIDTH).astype(x.dtype)

    qkv = jax.nn.silu(causal_depthwise_conv(qkv_b, dn_conv)).astype(f32)
    qb, kb, vb = jnp.split(qkv, 3, axis=-1)
    qb = l2norm(qb.reshape(B, S, DN_HEADS, DN_HEAD_DIM)) * (DN_HEAD_DIM ** -0.5)
    kb = l2norm(kb.reshape(B, S, DN_HEADS, DN_HEAD_DIM))
    vb = vb.reshape(B, S, DN_HEADS, DN_HEAD_DIM)
    beta = jax.nn.sigmoid(beta_in.astype(f32))
    g = -jnp.exp(dn_A_log.astype(f32)) * jax.nn.softplus(a_in.astype(f32) + dn_dt_bias.astype(f32))
    ob = gated_delta_rule(qb, kb, vb, g, beta)
    ob = ob * lax.rsqrt(jnp.mean(jnp.square(ob), axis=-1, keepdims=True) + RMS_EPS) * dn_norm_gain.astype(f32)
    ob = ob.reshape(B, S, DN_WIDTH) * jax.nn.silu(gate_b.astype(f32))
    y_b = ob.astype(x.dtype)

    gates = jax.nn.sigmoid(br_gate.astype(f32)).reshape(B, S, N_BRANCHES, D_MODEL)
    merged = gates[:, :, 0] * (y_a @ w_branch_attn).astype(f32) + gates[:, :, 1] * (y_b @ w_branch_delta).astype(f32)
    return merged.astype(x.dtype) @ w_out


def peer_sublayer(x, w_query, sub_keys, expert_down, expert_up):
    B, S, D = x.shape
    T = B * S
    t = x.reshape(T, D)
    q = (t @ w_query).reshape(T, PEER_HEADS, 2, PEER_HALF)
    s = jnp.einsum('thpc,hpkc->thpk', q, sub_keys).astype(f32)
    top_s, top_i = lax.top_k(s, PEER_TOPK)
    cand_s = (top_s[:, :, 0, :, None] + top_s[:, :, 1, None, :]).reshape(T, PEER_HEADS, PEER_TOPK * PEER_TOPK)
    cand_i = (top_i[:, :, 0, :, None] * N_KEYS + top_i[:, :, 1, None, :]).reshape(T, PEER_HEADS, PEER_TOPK * PEER_TOPK)
    best_s, best_j = lax.top_k(cand_s, PEER_TOPK)
    experts = jnp.take_along_axis(cand_i, best_j, axis=-1)
    gate = jax.nn.softmax(best_s, axis=-1)

    def token_block(args):
        xb, eb, gb = args
        act = jax.nn.gelu(jnp.einsum('pd,phkd->phk', xb, expert_down[eb]).astype(f32), approximate=False)
        return jnp.einsum('phk,phkd->pd', (gb * act).astype(x.dtype), expert_up[eb])

    nblk = T // PEER_TOKEN_BLOCK
    y = lax.map(token_block, (t.reshape(nblk, PEER_TOKEN_BLOCK, D),
                              experts.reshape(nblk, PEER_TOKEN_BLOCK, PEER_HEADS, PEER_TOPK),
                              gate.reshape(nblk, PEER_TOKEN_BLOCK, PEER_HEADS, PEER_TOPK)))
    return y.reshape(B, S, D)


def setup_inputs(seed: int = 0) -> dict:
    key = jax.random.key(seed)
    ks = jax.random.split(key, 18)
    nrm = jax.random.normal
    dt = jnp.exp(jax.random.uniform(ks[4], (DEPTH, DN_HEADS), f32) * (math.log(0.1) - math.log(0.001)) + math.log(0.001))
    return {
        "x": nrm(ks[0], (BATCH, SEQ, D_MODEL), f32),
        "w_in": nrm(ks[1], (DEPTH, D_MODEL, IN_COLS), f32) * D_MODEL ** -0.5,
        "dn_conv": nrm(ks[2], (DEPTH, CONV_WIDTH, 3 * DN_WIDTH), f32) * CONV_WIDTH ** -0.5,
        "dn_A_log": jnp.log(jax.random.uniform(ks[3], (DEPTH, DN_HEADS), f32, 1.0, 16.0)),
        "dn_dt_bias": dt + jnp.log(-jnp.expm1(-dt)),
        "dn_norm_gain": 1.0 + 0.02 * nrm(ks[5], (DEPTH, DN_HEAD_DIM), f32),
        "w_branch_attn": nrm(ks[6], (DEPTH, ATT_WIDTH, D_MODEL), f32) * ATT_WIDTH ** -0.5,
        "w_branch_delta": nrm(ks[7], (DEPTH, DN_WIDTH, D_MODEL), f32) * DN_WIDTH ** -0.5,
        "w_out": nrm(ks[8], (DEPTH, D_MODEL, D_MODEL), f32) * (D_MODEL ** -0.5) * BETA,
        "ln1_gain": 1.0 + 0.02 * nrm(ks[9], (DEPTH, D_MODEL), f32),
        "ln1_bias": 0.02 * nrm(ks[10], (DEPTH, D_MODEL), f32),
        "peer_w_query": nrm(ks[11], (DEPTH, D_MODEL, PEER_HEADS * PEER_QUERY_DIM), f32) * D_MODEL ** -0.5,
        "peer_sub_keys": nrm(ks[12], (DEPTH, PEER_HEADS, 2, N_KEYS, PEER_HALF), f32) * PEER_HALF ** -0.5,
        "peer_expert_down": nrm(ks[13], (DEPTH, N_EXPERTS, D_MODEL), f32) * D_MODEL ** -0.5,
        "peer_expert_up": nrm(ks[14], (DEPTH, N_EXPERTS, D_MODEL), f32) * BETA,
        "ln2_gain": 1.0 + 0.02 * nrm(ks[15], (DEPTH, D_MODEL), f32),
        "ln2_bias": 0.02 * nrm(ks[16], (DEPTH, D_MODEL), f32),
    }


def reference(x, w_in, dn_conv, dn_A_log, dn_dt_bias, dn_norm_gain, w_branch_attn, w_branch_delta,
              w_out, ln1_gain, ln1_bias, peer_w_query, peer_sub_keys, peer_expert_down, peer_expert_up,
              ln2_gain, ln2_bias):
    for layer in range(DEPTH):
        mix = mixing_sublayer(x, w_in[layer], dn_conv[layer], dn_A_log[layer], dn_dt_bias[layer],
                              dn_norm_gain[layer], w_branch_attn[layer], w_branch_delta[layer], w_out[layer])
        x = layer_norm(ALPHA * x + mix, ln1_gain[layer], ln1_bias[layer])
        ffn = peer_sublayer(x, peer_w_query[layer], peer_sub_keys[layer], peer_expert_down[layer],
                            peer_expert_up[layer])
        x = layer_norm(ALPHA * x + ffn, ln2_gain[layer], ln2_bias[layer])
    return x
```

```python
import functools
import math

import jax
import jax.numpy as jnp
from jax import lax
from jax.experimental import pallas as pl
from jax.experimental.pallas import tpu as pltpu

f32 = jnp.float32
bf16 = jnp.bfloat16

D_MODEL = 1024
SEQ = 2048
LANES = 128
SUBLANES = 8

ATT_HEADS = 12
ATT_HEAD_DIM = 64
ATT_WIDTH = ATT_HEADS * ATT_HEAD_DIM
ATT_BLOCK = 128
DILATIONS = (1, 4, 16)
ATT_BAND = 128
LAST_DIL = DILATIONS[-1]
ROPE_THETA = 10000.0

DN_HEADS = 6
DN_HEAD_DIM = 128
DN_WIDTH = DN_HEADS * DN_HEAD_DIM
CONV_WIDTH = 4
DN_CHUNK = 64
N_CHUNKS = SEQ // DN_CHUNK

PEER_HEADS = 8
N_KEYS = 128
PEER_HALF = 128
PEER_TOPK = 16

ALPHA = 2.0 ** 0.25
LN_EPS = 1e-5
RMS_EPS = 1e-6
NEG = -1e30

COL_BR = 0
COL_QA, COL_KA, COL_VA = 16, 22, 28
COL_QB, COL_KB, COL_VB = 34, 40, 46
COL_GATE_B = 52
N_COL_BLOCKS = 58
IN_COLS_PACKED = N_COL_BLOCKS * LANES

VMEM_LIMIT = 48 * 1024 * 1024


def _cparams(sem):
    return pltpu.CompilerParams(dimension_semantics=sem, vmem_limit_bytes=VMEM_LIMIT)


def _inproj_kernel(x_ref, w_ref, o_ref):
    o_ref[...] = jnp.dot(x_ref[...].astype(bf16), w_ref[...],
                         preferred_element_type=f32).astype(o_ref.dtype)


def _inproj(x2d, w_packed, out_dtype, name="inproj"):
    T = x2d.shape[0]
    n_cols = w_packed.shape[1]
    tm = 1024
    tn = n_cols // 2 if n_cols > 2048 else n_cols
    return pl.pallas_call(
        _inproj_kernel,
        grid=(T // tm, n_cols // tn),
        in_specs=[pl.BlockSpec((tm, D_MODEL), lambda i, j: (i, 0)),
                  pl.BlockSpec((D_MODEL, tn), lambda i, j: (0, j))],
        out_specs=pl.BlockSpec((tm, tn), lambda i, j: (i, j)),
        out_shape=jax.ShapeDtypeStruct((T, n_cols), out_dtype),
        compiler_params=_cparams(("parallel", "arbitrary")),
        name=name,
    )(x2d, w_packed)


def _attn_kernel(q_ref, k_ref, v_ref, cos_ref, sin_ref, o_ref,
                 qd, kd, vd, acc, mlrun):
    lane = lax.broadcasted_iota(jnp.int32, (1, LANES), 1)
    first_half = (lane % ATT_HEAD_DIM) < (ATT_HEAD_DIM // 2)
    head0 = lane < ATT_HEAD_DIM

    def rope(x):
        fwd = pltpu.roll(x, ATT_HEAD_DIM // 2, axis=1)
        bwd = pltpu.roll(x, LANES - ATT_HEAD_DIM // 2, axis=1)
        rot = jnp.where(first_half, bwd, fwd)
        return x * cos_ref[...] + rot * sin_ref[...]

    qr = rope(q_ref[...].astype(f32)) * (ATT_HEAD_DIM ** -0.5)
    kr = rope(k_ref[...].astype(f32))
    acc[0] = qr
    acc[1] = kr
    acc[2] = v_ref[...].astype(f32)

    zeros_blk = jnp.zeros((ATT_BLOCK, LANES), bf16)
    row_off = 0
    offs = []
    for g, r in enumerate(DILATIONS):
        L = SEQ // r
        offs.append(row_off)
        for c in range(r):
            src = pl.ds(c, L, stride=r) if r > 1 else pl.ds(0, L)
            qd[g, c * L:(c + 1) * L, :] = acc[0, src, :].astype(bf16)
            base = row_off + c * (ATT_BLOCK + L)
            kd[base:base + ATT_BLOCK, :] = zeros_blk
            vd[base:base + ATT_BLOCK, :] = zeros_blk
            kd[base + ATT_BLOCK:base + ATT_BLOCK + L, :] = acc[1, src, :].astype(bf16)
            vd[base + ATT_BLOCK:base + ATT_BLOCK + L, :] = acc[2, src, :].astype(bf16)
        row_off += r * (ATT_BLOCK + L)

    qi = lax.broadcasted_iota(jnp.int32, (ATT_BLOCK, 2 * ATT_BLOCK), 0)
    kj = lax.broadcasted_iota(jnp.int32, (ATT_BLOCK, 2 * ATT_BLOCK), 1)
    dist = ATT_BLOCK + qi - kj
    band_ok = (dist >= 0) & (dist <= ATT_BAND)
    band_ok2 = jnp.concatenate([band_ok, band_ok], axis=0)
    first_ok2 = band_ok2 & (jnp.concatenate([kj, kj], axis=0) >= ATT_BLOCK)

    for g, r in enumerate(DILATIONS):
        L = SEQ // r
        nb = L // ATT_BLOCK
        koff = offs[g]

        def block(c, n, g=g, r=r, L=L, koff=koff):
            q0 = c * L + n * ATT_BLOCK
            k0 = koff + c * (ATT_BLOCK + L) + n * ATT_BLOCK
            qb = qd[g, q0:q0 + ATT_BLOCK, :]
            kb = kd[k0:k0 + 2 * ATT_BLOCK, :]
            vb = vd[k0:k0 + 2 * ATT_BLOCK, :]
            valid = band_ok2 if n > 0 else first_ok2
            zq = jnp.zeros_like(qb)
            q2 = jnp.concatenate([jnp.where(head0, qb, zq), jnp.where(head0, zq, qb)], axis=0)
            s = lax.dot_general(q2, kb, (((1,), (1,)), ((), ())), preferred_element_type=f32)
            s = jnp.where(valid, s, NEG)
            m = jnp.max(s, axis=-1, keepdims=True)
            p = jnp.exp(s - m)
            l = jnp.sum(p, axis=-1, keepdims=True)
            o2 = jnp.dot(p.astype(bf16), vb, preferred_element_type=f32)
            o = jnp.where(head0, o2[:ATT_BLOCK], o2[ATT_BLOCK:])
            mf = jnp.where(head0, m[:ATT_BLOCK], m[ATT_BLOCK:])
            lf = jnp.where(head0, l[:ATT_BLOCK], l[ATT_BLOCK:])
            ml = jnp.where(first_half, mf, lf)
            if r == LAST_DIL:
                dst = pl.ds(q0, ATT_BLOCK)
            elif r == 1:
                dst = pl.ds(n * ATT_BLOCK, ATT_BLOCK)
            else:
                dst = pl.ds(c + n * (ATT_BLOCK * r), ATT_BLOCK, stride=r)
            acc[g, dst, :] = o
            mlrun[g, dst, :] = ml

        for c in range(r):
            for n in range(nb):
                block(c, n)

    n_pat = len(DILATIONS)
    l_last = SEQ // LAST_DIL

    def combine(i, carry):
        rows = pl.ds(pl.multiple_of(i * ATT_BLOCK, ATT_BLOCK), ATT_BLOCK)

        def natural(ref, g):
            if DILATIONS[g] != LAST_DIL:
                return ref[g, rows, :]
            parts = []
            for a in range(ATT_BLOCK // SUBLANES):
                t0 = a * SUBLANES
                start = (t0 % LAST_DIL) * l_last + i * (ATT_BLOCK // LAST_DIL) + t0 // LAST_DIL
                parts.append(ref[g, pl.ds(start, SUBLANES, stride=l_last), :])
            return jnp.concatenate(parts, axis=0)

        mls = [natural(mlrun, g) for g in range(n_pat)]
        ms = [jnp.where(first_half, x, pltpu.roll(x, ATT_HEAD_DIM // 2, axis=1)) for x in mls]
        ls = [jnp.where(first_half, pltpu.roll(x, LANES - ATT_HEAD_DIM // 2, axis=1), x) for x in mls]
        mmax = jnp.maximum(jnp.maximum(ms[0], ms[1]), ms[2])
        num = jnp.zeros((ATT_BLOCK, LANES), f32)
        den = jnp.zeros((ATT_BLOCK, LANES), f32)
        for g in range(n_pat):
            e = jnp.exp(ms[g] - mmax)
            num = num + e * natural(acc, g)
            den = den + e * ls[g]
        o_ref[rows, :] = num / den
        return carry

    lax.fori_loop(0, SEQ // ATT_BLOCK, combine, 0)


def _attention(h, cos_t, sin_t, B):
    T = h.shape[0]
    n_pairs = ATT_HEADS // 2
    kv_rows = sum(r * (ATT_BLOCK + SEQ // r) for r in DILATIONS)
    blk = lambda off: pl.BlockSpec((SEQ, LANES), lambda b, p, off=off: (b, off + p))
    tab = pl.BlockSpec((SEQ, LANES), lambda b, p: (0, 0))
    return pl.pallas_call(
        _attn_kernel,
        grid=(B, n_pairs),
        in_specs=[blk(COL_QA), blk(COL_KA), blk(COL_VA), tab, tab],
        out_specs=pl.BlockSpec((SEQ, LANES), lambda b, p: (b, p)),
        out_shape=jax.ShapeDtypeStruct((T, ATT_WIDTH), f32),
        scratch_shapes=[pltpu.VMEM((len(DILATIONS), SEQ, LANES), bf16),
                        pltpu.VMEM((kv_rows, LANES), bf16),
                        pltpu.VMEM((kv_rows, LANES), bf16),
                        pltpu.VMEM((len(DILATIONS), SEQ, LANES), f32),
                        pltpu.VMEM((len(DILATIONS), SEQ, LANES), f32)],
        compiler_params=_cparams(("parallel", "parallel")),
        name="attn",
    )(h, h, h, cos_t, sin_t)


def _gates_kernel(z_ref, alog_ref, dt_ref, col_ref, row_ref):
    z = z_ref[...]
    lane = lax.broadcasted_iota(jnp.int32, z.shape, 1)
    beta = jax.nn.sigmoid(z)
    g = -jnp.exp(alog_ref[...]) * jax.nn.softplus(z + dt_ref[...])
    pos = lax.broadcasted_iota(jnp.int32, z.shape, 0) % DN_CHUNK
    s = 1
    while s < DN_CHUNK:
        g = g + jnp.where(pos >= s, pltpu.roll(g, s, axis=0), 0.0)
        s *= 2
    col = jnp.where(lane < DN_HEADS, beta, g)
    col_ref[...] = col
    row_ref[...] = col.T


def _gates(h, alog_row, dt_row, B):
    T = h.shape[0]
    par = pl.BlockSpec((1, LANES), lambda b: (0, 0))
    return pl.pallas_call(
        _gates_kernel,
        grid=(B,),
        in_specs=[pl.BlockSpec((SEQ, LANES), lambda b: (b, 0)), par, par],
        out_specs=[pl.BlockSpec((SEQ, LANES), lambda b: (b, 0)),
                   pl.BlockSpec((LANES, SEQ), lambda b: (0, b))],
        out_shape=[jax.ShapeDtypeStruct((T, LANES), f32),
                   jax.ShapeDtypeStruct((LANES, T), f32)],
        compiler_params=_cparams(("parallel",)),
        name="gates",
    )(h, alog_row, dt_row)


DN_GROUP = 4
DN_ROWS = DN_GROUP * DN_CHUNK
DN_HPS = 2


def _dn_kernel(qp_ref, kp_ref, vp_ref, wq_ref, wk_ref, wv_ref, gcol_ref, grow_ref,
               gate_ref, gain_ref, o_ref,
               pad, qs, ks, vs, us, ws, qds, kds, qks, os_):
    pid = pl.program_id(1)
    lane = lax.broadcasted_iota(jnp.int32, (1, LANES), 1)
    C = DN_CHUNK
    R = DN_ROWS

    def conv_silu(x_ref, w_ref, lanes):
        pad[0:SUBLANES, :] = jnp.zeros((SUBLANES, LANES), f32)
        pad[SUBLANES:SUBLANES + SEQ, :] = x_ref[:, lanes].astype(f32)
        w = w_ref[:, lanes]
        y = w[CONV_WIDTH - 1:CONV_WIDTH, :] * pad[SUBLANES:SUBLANES + SEQ, :]
        for j in range(CONV_WIDTH - 1):
            sh = CONV_WIDTH - 1 - j
            y = y + w[j:j + 1, :] * pad[SUBLANES - sh:SUBLANES - sh + SEQ, :]
        return y * jax.nn.sigmoid(y)

    def l2norm(x):
        return x * lax.rsqrt(jnp.sum(x * x, axis=-1, keepdims=True) + RMS_EPS)

    def head_col(rows, lane_idx):
        return jnp.sum(jnp.where(lane == lane_idx, gcol_ref[rows, :], 0.0), axis=-1, keepdims=True)

    ri = lax.broadcasted_iota(jnp.int32, (R, R), 0)
    ci = lax.broadcasted_iota(jnp.int32, (R, R), 1)
    same = (ri // C) == (ci // C)
    lower = same & (ri >= ci)
    strict = same & (ri > ci)
    eye = jnp.where(ri == ci, 1.0, 0.0).astype(f32)

    def mm(a, b):
        return jnp.dot(a.astype(bf16), b.astype(bf16), preferred_element_type=f32)

    def mm_nt(a, b):
        return lax.dot_general(a.astype(bf16), b.astype(bf16), (((1,), (1,)), ((), ())),
                               preferred_element_type=f32)

    for hl in range(DN_HPS):
        hd = pid * DN_HPS + hl
        lanes = slice(hl * LANES, (hl + 1) * LANES)
        qs[...] = l2norm(conv_silu(qp_ref, wq_ref, lanes)) * (DN_HEAD_DIM ** -0.5)
        ks[...] = l2norm(conv_silu(kp_ref, wk_ref, lanes))
        vs[...] = conv_silu(vp_ref, wv_ref, lanes)
        for grp in range(SEQ // R):
            rows = slice(grp * R, (grp + 1) * R)
            qc, kc, vc = qs[rows, :], ks[rows, :], vs[rows, :]
            bc = head_col(rows, hd)
            gc = head_col(rows, hd + DN_HEADS)
            gr = grow_ref[hd + DN_HEADS, grp:grp + 1, :]
            decay = jnp.exp(jnp.where(lower, gc - gr, NEG))
            kq = mm_nt(jnp.concatenate([kc, qc], axis=0), kc)
            m = jnp.where(strict, bc * kq[:R] * decay, 0.0)
            t = eye - m
            p = mm(m, m)
            for _ in range(4):
                tp = mm(jnp.concatenate([t, p], axis=0), p)
                t = t + tp[:R]
                p = tp[R:]
            t = t + mm(t, p)
            eg = jnp.exp(gc)
            uw = mm(t, jnp.concatenate([bc * vc, bc * kc * eg], axis=1))
            us[hl, rows, :] = uw[:, :DN_HEAD_DIM]
            ws[hl, rows, :] = uw[:, DN_HEAD_DIM:]
            qk = kq[R:] * decay
            qkc = qk[:, 0:C]
            for cc in range(1, DN_GROUP):
                qkc = qkc + qk[:, cc * C:(cc + 1) * C]
            qks[hl, rows, :] = qkc
            qds[hl, rows, :] = qc * eg
            for cc in range(DN_GROUP):
                lo = grp * R + cc * C
                g_last = head_col(slice(lo + C - 1, lo + C), hd + DN_HEADS)
                kds[hl, lo:lo + C, :] = (kc[cc * C:(cc + 1) * C]
                                         * jnp.exp(g_last - gc[cc * C:(cc + 1) * C]))

    def block_diag(blocks):
        rows = []
        for i, blk in enumerate(blocks):
            z = jnp.zeros_like(blk)
            rows.append(jnp.concatenate([blk if j == i else z for j in range(len(blocks))], axis=1))
        return jnp.concatenate(rows, axis=0)

    D = DN_HEAD_DIM
    states = [jnp.zeros((D, D), f32) for _ in range(DN_HPS)]
    for ch in range(N_CHUNKS):
        rows = slice(ch * C, (ch + 1) * C)
        heads = range(DN_HPS)
        sb = jnp.concatenate([states[hl].astype(bf16) for hl in heads], axis=0)
        wq = block_diag([jnp.concatenate([ws[hl, rows, :], qds[hl, rows, :]], axis=0).astype(bf16)
                         for hl in heads])
        ws_qs = jnp.dot(wq, sb, preferred_element_type=f32)
        v_new = [us[hl, rows, :] - ws_qs[2 * C * hl:2 * C * hl + C] for hl in heads]
        vb = jnp.concatenate(v_new, axis=0).astype(bf16)
        qkv = jnp.dot(block_diag([qks[hl, rows, :].astype(bf16) for hl in heads]), vb,
                      preferred_element_type=f32)
        upd = lax.dot_general(block_diag([kds[hl, rows, :].astype(bf16) for hl in heads]), vb,
                              (((0,), (0,)), ((), ())), preferred_element_type=f32)
        for hl in heads:
            os_[hl, rows, :] = ws_qs[2 * C * hl + C:2 * C * (hl + 1)] + qkv[C * hl:C * (hl + 1)]
            g_last = jnp.exp(head_col(slice(ch * C + C - 1, ch * C + C),
                                      pid * DN_HPS + hl + DN_HEADS))
            states[hl] = g_last * states[hl] + upd[D * hl:D * (hl + 1)]

    for hl in range(DN_HPS):
        lanes = slice(hl * LANES, (hl + 1) * LANES)
        ob = os_[hl]
        ob = ob * lax.rsqrt(jnp.mean(ob * ob, axis=-1, keepdims=True) + RMS_EPS) * gain_ref[...]
        gt = gate_ref[:, lanes].astype(f32)
        o_ref[:, lanes] = ob * (gt * jax.nn.sigmoid(gt))


def _deltanet(h, conv_w, gcol, grow3, gain_row, B):
    T = h.shape[0]
    W = DN_HPS * LANES
    blk = lambda off: pl.BlockSpec((SEQ, W), lambda b, p, off=off: (b, off // DN_HPS + p))
    cw = lambda off: pl.BlockSpec((CONV_WIDTH, W), lambda b, p, off=off: (0, off // DN_HPS + p))
    vm = lambda shape: pltpu.VMEM(shape, f32)
    per_head = (DN_HPS, SEQ, LANES)
    return pl.pallas_call(
        _dn_kernel,
        grid=(B, DN_HEADS // DN_HPS),
        in_specs=[blk(COL_QB), blk(COL_KB), blk(COL_VB), cw(0), cw(DN_HEADS), cw(2 * DN_HEADS),
                  pl.BlockSpec((SEQ, LANES), lambda b, p: (b, 0)),
                  pl.BlockSpec((LANES, SEQ // DN_ROWS, DN_ROWS), lambda b, p: (0, b, 0)),
                  blk(COL_GATE_B),
                  pl.BlockSpec((1, LANES), lambda b, p: (0, 0))],
        out_specs=pl.BlockSpec((SEQ, W), lambda b, p: (b, p)),
        out_shape=jax.ShapeDtypeStruct((T, DN_WIDTH), f32),
        scratch_shapes=[vm((SEQ + SUBLANES, LANES)), vm((SEQ, LANES)), vm((SEQ, LANES)),
                        vm((SEQ, LANES)), vm(per_head), vm(per_head), vm(per_head), vm(per_head),
                        vm((DN_HPS, SEQ, DN_CHUNK)), vm(per_head)],
        compiler_params=_cparams(("parallel", "parallel")),
        name="deltanet",
    )(h, h, h, conv_w, conv_w, conv_w, gcol, grow3, h, gain_row)


def _layer_norm(z, gain, bias):
    mu = jnp.mean(z, axis=-1, keepdims=True)
    d = z - mu
    var = jnp.mean(d * d, axis=-1, keepdims=True)
    return d * lax.rsqrt(var + LN_EPS) * gain + bias


def _merge_kernel(ya_ref, yb_ref, ga_ref, gb_ref, x_ref, wa_ref, wb_ref, wo_ref,
                  gain_ref, bias_ref, o_ref):
    pa = jnp.dot(ya_ref[...].astype(bf16), wa_ref[...], preferred_element_type=f32)
    pb = jnp.dot(yb_ref[...].astype(bf16), wb_ref[...], preferred_element_type=f32)
    merged = (jax.nn.sigmoid(ga_ref[...].astype(f32)) * pa
              + jax.nn.sigmoid(gb_ref[...].astype(f32)) * pb)
    mix = jnp.dot(merged.astype(bf16), wo_ref[...], preferred_element_type=f32)
    o_ref[...] = _layer_norm(ALPHA * x_ref[...] + mix, gain_ref[...], bias_ref[...])


def _merge(ya, yb, h, x2d, wa, wb, wo, gain, bias):
    T = x2d.shape[0]
    tm = 256
    nbr = D_MODEL // LANES
    row = lambda w: pl.BlockSpec((tm, w), lambda i: (i, 0))
    full = lambda a: pl.BlockSpec(a.shape, lambda i: (0, 0))
    return pl.pallas_call(
        _merge_kernel,
        grid=(T // tm,),
        in_specs=[row(ATT_WIDTH), row(DN_WIDTH),
                  pl.BlockSpec((tm, D_MODEL), lambda i: (i, COL_BR // nbr)),
                  pl.BlockSpec((tm, D_MODEL), lambda i: (i, COL_BR // nbr + 1)),
                  row(D_MODEL), full(wa), full(wb), full(wo), full(gain), full(bias)],
        out_specs=row(D_MODEL),
        out_shape=jax.ShapeDtypeStruct((T, D_MODEL), f32),
        compiler_params=_cparams(("parallel",)),
        name="merge",
    )(ya, yb, h, h, x2d, wa, wb, wo, gain, bias)


_CAND_ROWS = 80
ROUTE_HEADS = 4


def _topk_rank(s, k_iota):
    return _topk_rank_many([s], k_iota)[0]


def _topk_rank_many(scores, k_iota):
    n = len(scores)
    scores = list(scores)
    ranks = [jnp.full(s.shape, float(PEER_TOPK), f32) for s in scores]
    tops = [[] for _ in scores]
    for k in range(PEER_TOPK):
        for c in range(n):
            s = scores[c]
            m = jnp.max(s, axis=0, keepdims=True)
            idx = jnp.min(jnp.where(s == m, k_iota, float(N_KEYS)), axis=0, keepdims=True)
            hit = k_iota == idx
            ranks[c] = jnp.where(hit, float(k), ranks[c])
            scores[c] = jnp.where(hit, -jnp.inf, s)
            tops[c].append(m)
    return [(jnp.concatenate(t, axis=0), r) for t, r in zip(tops, ranks)]


def _peer_route_kernel(x_ref, wq_ref, keys_ref, rank1_ref, e1_ref, lim_ref, w0_ref, xb, sc, *, tm):
    xb[...] = x_ref[...].astype(bf16)
    n_tc = tm // LANES

    def heads(g, carry):
        for hl in range(ROUTE_HEADS):
            hh = g * ROUTE_HEADS + hl
            qt = lax.dot_general(wq_ref[hh], xb[...], (((1,), (1,)), ((), ())),
                                 preferred_element_type=f32)
            for p in range(2):
                qp = qt[p * PEER_HALF:(p + 1) * PEER_HALF, :].astype(bf16)
                sc[hl, p] = jnp.dot(keys_ref[hh, p], qp, preferred_element_type=f32)
        where = [(hl, slice(tc * LANES, (tc + 1) * LANES))
                 for hl in range(ROUTE_HEADS) for tc in range(n_tc)]

        def store(outs):
            for (hl, c), vals in zip(where, outs):
                for ref, val in zip((rank1_ref, e1_ref, lim_ref, w0_ref), vals):
                    ref[g * ROUTE_HEADS + hl, :, c] = val

        outs, tied = _route_chunks_distinct([sc[hl, 0, :, c] for hl, c in where],
                                            [sc[hl, 1, :, c] for hl, c in where])
        store(outs)

        @pl.when(jnp.max(tied) > 0.0)
        def _():
            store(_route_chunks([sc[hl, 0, :, c] for hl, c in where],
                                [sc[hl, 1, :, c] for hl, c in where]))

        return carry

    lax.fori_loop(0, PEER_HEADS // ROUTE_HEADS, heads, 0)


def _route_chunk(s0, s1):
    return _route_chunks([s0], [s1])[0]


def _candidates(t0, t1, b8):
    pieces = [t0[0:1, :] + t1]
    for a in range(1, SUBLANES):
        bound = PEER_TOPK // (a + 1)
        pieces.append(jnp.where(b8 < bound, t0[a:a + 1, :] + t1[0:SUBLANES, :], -jnp.inf))
    pieces.append(t0[SUBLANES:PEER_TOPK, :] + t1[0:1, :])
    return jnp.concatenate(pieces, axis=0)


def _routing_factors(s0, s1, t0, r0, t1, r1, cnt, zsum):
    lim = jnp.zeros(s0.shape, f32)
    for a in range(PEER_TOPK):
        lim = jnp.where(r0 == float(a), cnt[a:a + 1, :], lim)
    w0 = jnp.where(r0 < PEER_TOPK, jnp.exp(s0 - t0[0:1, :]), 0.0) * (0.5 / zsum)
    e1 = jnp.where(r1 < PEER_TOPK, jnp.exp(s1 - t1[0:1, :]), 0.0)
    return r1, e1, lim, w0


def _topk_rank_distinct(scores):
    work = list(scores)
    ranks = [jnp.full(s.shape, float(PEER_TOPK), f32) for s in scores]
    tops = [[] for _ in scores]
    for k in range(PEER_TOPK):
        for c in range(len(scores)):
            s = work[c]
            m = jnp.max(s, axis=0, keepdims=True)
            hit = s == m
            ranks[c] = jnp.where(hit, float(k), ranks[c])
            work[c] = jnp.where(hit, -jnp.inf, s)
            tops[c].append(m)
    tied = jnp.zeros((1, scores[0].shape[1]), f32)
    for r in ranks:
        taken = jnp.sum(jnp.where(r < PEER_TOPK, 1.0, 0.0), axis=0, keepdims=True)
        tied = tied + jnp.where(taken == float(PEER_TOPK), 0.0, 1.0)
    return [(jnp.concatenate(t, axis=0), r) for t, r in zip(tops, ranks)], tied


def _route_chunks_distinct(s0s, s1s):
    res, tied = _topk_rank_distinct(list(s0s) + list(s1s))
    return _route_second_stage(s0s, s1s, res), tied


def _route_chunks(s0s, s1s):
    k_iota = lax.broadcasted_iota(jnp.int32, (N_KEYS, LANES), 0).astype(f32)
    return _route_second_stage(s0s, s1s, _topk_rank_many(list(s0s) + list(s1s), k_iota))


def _route_second_stage(s0s, s1s, res):
    tm = LANES
    n = len(s0s)
    b16 = lax.broadcasted_iota(jnp.int32, (PEER_TOPK, tm), 0).astype(f32)
    b8 = lax.broadcasted_iota(jnp.int32, (SUBLANES, tm), 0).astype(f32)
    cands, cnts, zsums, best0s = [], [], [], []
    flats = [b16] + [a * PEER_TOPK + b8 for a in range(1, SUBLANES)] + [(b8 + SUBLANES) * PEER_TOPK]
    flat = jnp.concatenate(flats, axis=0)
    for c in range(n):
        cands.append(_candidates(res[c][0], res[n + c][0], b8))
        cnts.append(jnp.zeros((PEER_TOPK, tm), f32))
        zsums.append(jnp.zeros((1, tm), f32))
        best0s.append(None)
    for k in range(PEER_TOPK):
        for c in range(n):
            cand = cands[c]
            m = jnp.max(cand, axis=0, keepdims=True)
            fidx = jnp.min(jnp.where(cand == m, flat, float(PEER_TOPK * PEER_TOPK)),
                           axis=0, keepdims=True)
            cands[c] = jnp.where(flat == fidx, -jnp.inf, cand)
            a_sel = jnp.floor(fidx * (1.0 / PEER_TOPK))
            cnts[c] = cnts[c] + jnp.where(b16 == a_sel, 1.0, 0.0)
            if k == 0:
                best0s[c] = m
            zsums[c] = zsums[c] + jnp.exp(m - best0s[c])
    outs = []
    for c in range(n):
        (t0, r0), (t1, r1) = res[c], res[n + c]
        outs.append(_routing_factors(s0s[c], s1s[c], t0, r0, t1, r1, cnts[c], zsums[c]))
    return outs


def _peer_route(x1, wq_t, keys):
    T = x1.shape[0]
    tm = 256
    out = jax.ShapeDtypeStruct((PEER_HEADS, N_KEYS, T), f32)
    ospec = pl.BlockSpec((PEER_HEADS, N_KEYS, tm), lambda i: (0, 0, i))
    return pl.pallas_call(
        functools.partial(_peer_route_kernel, tm=tm),
        grid=(T // tm,),
        in_specs=[pl.BlockSpec((tm, D_MODEL), lambda i: (i, 0)),
                  pl.BlockSpec(wq_t.shape, lambda i: (0, 0, 0)),
                  pl.BlockSpec(keys.shape, lambda i: (0, 0, 0, 0))],
        out_specs=[ospec, ospec, ospec, ospec],
        out_shape=[out, out, out, out],
        scratch_shapes=[pltpu.VMEM((tm, D_MODEL), bf16),
                        pltpu.VMEM((ROUTE_HEADS, 2, N_KEYS, tm), f32)],
        compiler_params=_cparams(("parallel",)),
        name="peer_route",
    )(x1, wq_t, keys)


PEER_TM = 512
PEER_TE = 2048
PEER_CHUNK = 512


def _peer_ffn_kernel(x_ref, ed_ref, eut_ref, rank1_ref, e1_ref, lim_ref, w0_ref,
                     gain_ref, bias_ref, o_ref, xb, ht, pt, acct, r16, e16):
    ei = pl.program_id(1)

    @pl.when(ei == 0)
    def _():
        xb[...] = x_ref[...].astype(bf16)
        acct[...] = jnp.zeros_like(acct)
        r16[...] = rank1_ref[...].astype(bf16)
        e16[...] = e1_ref[...].astype(bf16)

    zero = jnp.zeros((N_KEYS, LANES), bf16)
    n_chunks = PEER_TE // PEER_CHUNK
    ib_per_chunk = PEER_CHUNK // N_KEYS
    half_w = PEER_TM // 2

    def erows(ch):
        return slice(ch * PEER_CHUNK, (ch + 1) * PEER_CHUNK)

    def tcols(half):
        return slice(half * half_w, (half + 1) * half_w)

    def route(ibs, tc):
        cols = slice(tc * LANES, (tc + 1) * LANES)
        cgs = [zero for _ in ibs]
        for hh in range(PEER_HEADS):
            r1, e1 = r16[hh, :, cols], e16[hh, :, cols]
            for n, ib in enumerate(ibs):
                lim = jnp.broadcast_to(lim_ref[hh, ib:ib + 1, cols], (N_KEYS, LANES)).astype(bf16)
                w0 = jnp.broadcast_to(w0_ref[hh, ib:ib + 1, cols], (N_KEYS, LANES)).astype(bf16)
                cgs[n] = cgs[n] + w0 * jnp.where(r1 < lim, e1, zero)
        for n, ib in enumerate(ibs):
            pt[ib * N_KEYS:(ib + 1) * N_KEYS, cols] = cgs[n]

    def activate(ib, tc):
        rows = slice(ib * N_KEYS, (ib + 1) * N_KEYS)
        cols = slice(tc * LANES, (tc + 1) * LANES)
        hblk = ht[rows, cols]
        act = hblk * (1.0 + lax.erf(hblk * (2.0 ** -0.5)))
        pt[rows, cols] = pt[rows, cols] * act.astype(bf16)

    ht[...] = lax.dot_general(ed_ref[...], xb[...], (((1,), (1,)), ((), ())),
                              preferred_element_type=f32)
    n_ib = PEER_TE // N_KEYS
    for pair in range(n_ib // 2):
        for tc in range(PEER_TM // LANES):
            route((2 * pair, 2 * pair + 1), tc)
    for ib in range(n_ib):
        for tc in range(PEER_TM // LANES):
            activate(ib, tc)
    acct[...] += jnp.dot(eut_ref[...], pt[...], preferred_element_type=f32)

    @pl.when(ei == pl.num_programs(1) - 1)
    def _():
        y = acct[...].T
        o_ref[...] = _layer_norm(ALPHA * x_ref[...] + y, gain_ref[...], bias_ref[...])


def _peer_ffn(x1, ed, eut, rank1, e1, lim, w0, gain, bias):
    T = x1.shape[0]
    n_exp = ed.shape[0]
    tm, te = PEER_TM, PEER_TE
    nib = te // N_KEYS
    tok = pl.BlockSpec((PEER_HEADS, N_KEYS, tm), lambda i, e: (0, 0, i))
    per_i = pl.BlockSpec((PEER_HEADS, nib, tm), lambda i, e: (0, e, i))
    par = pl.BlockSpec((1, D_MODEL), lambda i, e: (0, 0))
    return pl.pallas_call(
        _peer_ffn_kernel,
        grid=(T // tm, n_exp // te),
        in_specs=[pl.BlockSpec((tm, D_MODEL), lambda i, e: (i, 0)),
                  pl.BlockSpec((te, D_MODEL), lambda i, e: (e, 0)),
                  pl.BlockSpec((D_MODEL, te), lambda i, e: (0, e)),
                  tok, tok, per_i, per_i, par, par],
        out_specs=pl.BlockSpec((tm, D_MODEL), lambda i, e: (i, 0)),
        out_shape=jax.ShapeDtypeStruct((T, D_MODEL), f32),
        scratch_shapes=[pltpu.VMEM((tm, D_MODEL), bf16),
                        pltpu.VMEM((te, tm), f32),
                        pltpu.VMEM((te, tm), bf16),
                        pltpu.VMEM((D_MODEL, tm), f32),
                        pltpu.VMEM((PEER_HEADS, N_KEYS, tm), bf16),
                        pltpu.VMEM((PEER_HEADS, N_KEYS, tm), bf16)],
        compiler_params=_cparams(("parallel", "arbitrary")),
        name="peer_ffn",
    )(x1, ed, eut, rank1, e1, lim, w0, gain, bias)


def _rope_tables():
    d = ATT_HEAD_DIM
    inv_freq = ROPE_THETA ** (-jnp.arange(0, d, 2, dtype=f32) / d)
    ang = jnp.arange(SEQ, dtype=f32)[:, None] * inv_freq[None, :]
    cos, sin = jnp.cos(ang), jnp.sin(ang)
    cos_t = jnp.tile(cos, (1, LANES // (d // 2)))
    sin_t = jnp.tile(jnp.concatenate([-sin, sin], axis=1), (1, LANES // d))
    return cos_t, sin_t


def _layer(x, w_in, dn_conv, dn_A_log, dn_dt_bias, dn_norm_gain, w_branch_attn, w_branch_delta,
           w_out, ln1_gain, ln1_bias, peer_w_query, peer_sub_keys, peer_expert_down,
           peer_expert_up, ln2_gain, ln2_bias):
    B, S, D = x.shape
    T = B * S
    x2d = x.reshape(T, D)

    c_qkv_b = 3 * ATT_WIDTH
    c_beta = c_qkv_b + 3 * DN_WIDTH
    c_gate = c_beta + 2 * DN_HEADS
    c_br = c_gate + DN_WIDTH
    w_small = jnp.pad(w_in[:, c_beta:c_gate], ((0, 0), (0, LANES - 2 * DN_HEADS))).astype(bf16)
    w_packed = jnp.concatenate([w_in[:, c_br:], w_in[:, :c_beta], w_in[:, c_gate:c_br]],
                               axis=1).astype(bf16)

    h = _inproj(x2d, w_packed, bf16)
    h_small = _inproj(x2d, w_small, f32, name="inproj_gates")

    cos_t, sin_t = _rope_tables()
    y_a = _attention(h, cos_t, sin_t, B)

    pad6 = lambda v, off: jnp.zeros((1, LANES), f32).at[0, off:off + DN_HEADS].set(v.astype(f32))
    gcol, grow = _gates(h_small, pad6(dn_A_log, DN_HEADS), pad6(dn_dt_bias, DN_HEADS), B)
    grow3 = grow.reshape(LANES, T // DN_ROWS, DN_ROWS)
    y_b = _deltanet(h, dn_conv.astype(f32), gcol, grow3,
                    dn_norm_gain.astype(f32).reshape(1, DN_HEAD_DIM), B)

    x1 = _merge(y_a, y_b, h, x2d, w_branch_attn.astype(bf16), w_branch_delta.astype(bf16),
                w_out.astype(bf16), ln1_gain.reshape(1, D).astype(f32),
                ln1_bias.reshape(1, D).astype(f32))

    wq_t = peer_w_query.T.reshape(PEER_HEADS, 2 * PEER_HALF, D).astype(bf16)
    rank1, e1, lim, w0 = _peer_route(x1, wq_t, peer_sub_keys.astype(bf16))
    out = _peer_ffn(x1, peer_expert_down.astype(bf16), peer_expert_up.T.astype(bf16),
                    rank1, e1, lim, w0, ln2_gain.reshape(1, D).astype(f32),
                    ln2_bias.reshape(1, D).astype(f32))
    return out.reshape(B, S, D)


def kernel(x, w_in, dn_conv, dn_A_log, dn_dt_bias, dn_norm_gain, w_branch_attn, w_branch_delta,
           w_out, ln1_gain, ln1_bias, peer_w_query, peer_sub_keys, peer_expert_down,
           peer_expert_up, ln2_gain, ln2_bias):
    depth = w_in.shape[0]
    for layer in range(depth):
        x = _layer(x, w_in[layer], dn_conv[layer], dn_A_log[layer], dn_dt_bias[layer],
                   dn_norm_gain[layer], w_branch_attn[layer], w_branch_delta[layer], w_out[layer],
                   ln1_gain[layer], ln1_bias[layer], peer_w_query[layer], peer_sub_keys[layer],
                   peer_expert_down[layer], peer_expert_up[layer], ln2_gain[layer], ln2_bias[layer])
    return x
```
